```python
import math
import jax
import jax.numpy as jnp
from jax import lax
import numpy as np

D_MODEL = 1024
BATCH = 4
SEQ = 4096
DEPTH = 2

GRID_W = 64
CTX_LEN = 256
RMS_EPS = 1e-6
GN_EPS = 1e-5
N_BRANCH = 3
MIX_W = 512

RET_HEADS = 4
RET_DK = 128
RET_DV = MIX_W // RET_HEADS
RET_QK_W = RET_HEADS * RET_DK
RET_CHUNK = 128

S5_GROUP = 16
S5_GROUPS = MIX_W // S5_GROUP
S5_STATE = 64
S5_DT_MIN = 1e-3
S5_DT_MAX = 1e-1

HY_SHORT = 3
HY_EMB = 33
HY_BANDS = (HY_EMB - 1) // 2
HY_FHID = 64
HY_DECAY_TARGET = 1e-2
HY_FAST_PCT = 0.3
HY_SLOW_PCT = 1.5

MOE_GROUPS = 4
MOE_PER_GROUP = 8
MOE_EXPERTS = MOE_GROUPS * MOE_PER_GROUP
MOE_HIDDEN = 512
MOE_TOPK = 2
MOE_BLOCK = 128

COL_Q = 0
COL_K = COL_Q + RET_QK_W
COL_V = COL_K + RET_QK_W
COL_G = COL_V + MIX_W
COL_S5 = COL_G + MIX_W
COL_HY = COL_S5 + MIX_W
COL_GATE = COL_HY + 3 * MIX_W
IN_COLS = COL_GATE + N_BRANCH * D_MODEL

kernel_name = 'hybrid_retention_s5_hyena_hmoe_flow_block'


def rms_norm(x, g):
    xf = x.astype(jnp.float32)
    y = xf * lax.rsqrt(jnp.mean(xf * xf, axis=-1, keepdims=True) + RMS_EPS)
    return (y * g.astype(jnp.float32)).astype(x.dtype)


def grid_pos_embed(n_tokens, dtype):
    rows = n_tokens // GRID_W
    quarter = D_MODEL // 4
    omega = 1.0 / (10000.0 ** (jnp.arange(quarter, dtype=jnp.float32) / quarter))
    ang_r = jnp.arange(rows, dtype=jnp.float32)[:, None] * omega
    ang_c = jnp.arange(GRID_W, dtype=jnp.float32)[:, None] * omega
    emb_r = jnp.concatenate([jnp.sin(ang_r), jnp.cos(ang_r)], axis=-1)
    emb_c = jnp.concatenate([jnp.sin(ang_c), jnp.cos(ang_c)], axis=-1)
    half = D_MODEL // 2
    pe = jnp.concatenate([jnp.broadcast_to(emb_r[:, None, :], (rows, GRID_W, half)),
                          jnp.broadcast_to(emb_c[None, :, :], (rows, GRID_W, half))], axis=-1)
    return pe.reshape(rows * GRID_W, D_MODEL).astype(dtype)


def _heads(t):
    return t.astype(jnp.float32).reshape(t.shape[0], t.shape[1], RET_HEADS, -1)


def _ret_masks(log_gamma, include_diag):
    lg = log_gamma.astype(jnp.float32)
    idx = jnp.arange(RET_CHUNK, dtype=jnp.float32)
    diff = idx[:, None] - idx[None, :]
    keep = (diff >= 0) if include_diag else (diff > 0)
    inner = jnp.where(keep[None], jnp.exp(lg[:, None, None] * jnp.maximum(diff, 0.0)[None]), 0.0)
    q_dec = jnp.exp(lg[:, None] * (idx + 1.0)[None])
    k_dec = jnp.exp(lg[:, None] * (RET_CHUNK - 1.0 - idx)[None])
    c_dec = jnp.exp(lg * RET_CHUNK)
    return inner, q_dec, k_dec, c_dec


def retention_scan(q, k, v, log_gamma, s0, include_diag):
    bsz, n_tok = q.shape[0], q.shape[1]
    n_chunks = n_tok // RET_CHUNK

    def chunks(t):
        return t.reshape(bsz, n_chunks, RET_CHUNK, RET_HEADS, t.shape[-1]).transpose(1, 0, 3, 2, 4)

    inner, q_dec, k_dec, c_dec = _ret_masks(log_gamma, include_diag)

    def step(s, qkv):
        qi, ki, vi = qkv
        att = jnp.einsum('bhid,bhjd->bhij', qi, ki) * inner
        o = jnp.einsum('bhij,bhje->bhie', att, vi) + jnp.einsum('bhid,bhde->bhie', qi * q_dec[..., None], s)
        s = s * c_dec[:, None, None] + jnp.einsum('bhjd,bhje->bhde', ki * k_dec[..., None], vi)
        return s, o

    _, o = lax.scan(step, s0, (chunks(q), chunks(k), chunks(v)))
    return o.transpose(1, 0, 3, 2, 4).reshape(bsz, n_tok, RET_HEADS, RET_DV)


def retention_state(k, v, log_gamma):
    n_tok = k.shape[1]
    dist = (n_tok - 1 - jnp.arange(n_tok)).astype(jnp.float32)
    w = jnp.exp(dist[:, None] * log_gamma.astype(jnp.float32)[None])
    return jnp.einsum('blhd,lh,blhe->bhde', k, w, v)


def bi_retention(q, k, v, log_decay, s0_f, s0_b):
    flip = lambda t: jnp.flip(t, axis=1)
    o_f = retention_scan(q, k, v, log_decay[0], s0_f, True)
    o_b = retention_scan(flip(q), flip(k), flip(v), log_decay[1], s0_b, False)
    return o_f + flip(o_b)


def retention_readout(o, g, gn_g):
    mu = jnp.mean(o, axis=-1, keepdims=True)
    var = jnp.mean(jnp.square(o - mu), axis=-1, keepdims=True)
    on = ((o - mu) * lax.rsqrt(var + GN_EPS)).reshape(g.shape) * gn_g.astype(jnp.float32)
    return (on * jax.nn.silu(g.astype(jnp.float32))).astype(g.dtype)


def _s5_groups(t):
    return t.astype(jnp.float32).reshape(t.shape[0], t.shape[1], S5_GROUPS, S5_GROUP).astype(jnp.complex64)


def s5_discretize(lam_re, lam_im, log_dt, b_re, b_im):
    lam = lax.complex(lam_re.astype(jnp.float32), lam_im.astype(jnp.float32))
    lam_dt = lam * jnp.exp(log_dt.astype(jnp.float32))[:, None]
    a_bar = jnp.exp(lam_dt)
    bmat = lax.complex(b_re.astype(jnp.float32), b_im.astype(jnp.float32))
    b_bar = ((a_bar - 1.0) / lam)[..., None] * bmat
    return lam_dt, a_bar, b_bar


def s5_state(u, lam_dt, b_bar):
    n_tok = u.shape[1]
    dist = (n_tok - 1 - jnp.arange(n_tok)).astype(jnp.float32)
    pw = jnp.exp(dist[:, None, None] * lam_dt[None])
    return jnp.einsum('lgn,gnp,blgp->bgn', pw, b_bar, u)


def s5_scan(u, a_bar, b_bar, c, h0, reverse):
    bu = jnp.einsum('gnp,blgp->blgn', b_bar, u)
    if reverse:
        bu = jnp.flip(bu, axis=1)
    bu = bu.at[:, 0].add(a_bar[None] * h0)
    a = jnp.broadcast_to(a_bar, bu.shape)

    def combine(e1, e2):
        return e1[0] * e2[0], e2[0] * e1[1] + e2[1]

    _, h = lax.associative_scan(combine, (a, bu), axis=1)
    if reverse:
        h = jnp.flip(h, axis=1)
    return jnp.einsum('gpn,blgn->blgp', c, h).real


def s5_readout(y, u, d, glu_w1, glu_w2):
    yy = y.reshape(u.shape) + d.astype(jnp.float32) * u.astype(jnp.float32)
    g = jax.nn.gelu(yy)
    return ((g @ glu_w1.astype(jnp.float32)) * jax.nn.sigmoid(g @ glu_w2.astype(jnp.float32))).astype(u.dtype)


def s5_mixer(u, uc, lam_re, lam_im, log_dt, b_re, b_im, c_re, c_im, d, glu_w1, glu_w2, need_ctx):
    ul, ucl = _s5_groups(u), _s5_groups(uc)
    y = 0.0
    yc = 0.0
    for direction in range(2):
        rev = direction == 1
        lam_dt, a_bar, b_bar = s5_discretize(lam_re[direction], lam_im[direction], log_dt[direction],
                                             b_re[direction], b_im[direction])
        cmat = lax.complex(c_re[direction].astype(jnp.float32), c_im[direction].astype(jnp.float32))
        h0 = s5_state(jnp.flip(ucl, axis=1) if rev else ucl, lam_dt, b_bar)
        y = y + s5_scan(ul, a_bar, b_bar, cmat, h0, rev)
        if need_ctx:
            yc = yc + s5_scan(ucl, a_bar, b_bar, cmat, jnp.zeros_like(h0), rev)
    out = s5_readout(y, u, d, glu_w1, glu_w2)
    out_c = s5_readout(yc, uc, d, glu_w1, glu_w2) if need_ctx else None
    return out, out_c


def short_conv(z, w, b):
    zp = jnp.pad(z, ((0, 0), (1, 1), (0, 0)))
    return zp[:, :-2] * w[0] + zp[:, 1:-1] * w[1] + zp[:, 2:] * w[2] + b


def hyena_filter(n_tok, w1, b1, w2, b2, w3, freq):
    f32 = jnp.float32
    t = jnp.linspace(0.0, 1.0, n_tok, dtype=f32)[:, None]
    w = 2.0 * math.pi * jnp.arange(n_tok, dtype=f32)[:, None] / n_tok
    f = jnp.linspace(1e-4, HY_BANDS - 1, HY_BANDS, dtype=f32)[None]
    z = jnp.concatenate([t, jnp.cos(f * w), -jnp.sin(f * w)], axis=-1)
    freq = freq.astype(f32)
    h = jnp.sin(freq[0] * (z @ w1.astype(f32) + b1.astype(f32)))
    h = jnp.sin(freq[1] * (h @ w2.astype(f32) + b2.astype(f32)))
    h = h @ w3.astype(f32)
    offset = jnp.abs(jnp.arange(n_tok) - n_tok // 2).astype(f32) / n_tok
    deltas = jnp.abs(jnp.linspace(math.log(HY_DECAY_TARGET) / HY_SLOW_PCT,
                                  math.log(HY_DECAY_TARGET) / HY_FAST_PCT, MIX_W, dtype=f32))
    return h * jnp.exp(-offset[:, None] * deltas[None])


def long_conv_centred(u, h):
    n_tok = u.shape[1]
    n_fft = 2 * n_tok
    y = jnp.fft.irfft(jnp.fft.rfft(u, n=n_fft, axis=1) * jnp.fft.rfft(h, n=n_fft, axis=0)[None], n=n_fft, axis=1)
    return y[:, n_tok // 2: n_tok // 2 + n_tok]


def hyena_seq(z, short_w, short_b, filt, bias):
    zs = short_conv(z, short_w, short_b).astype(jnp.float32)
    x0, x1, v = jnp.split(zs, 3, axis=-1)
    y = x1 * v
    y = long_conv_centred(y, filt) + y * bias.astype(jnp.float32)
    return (x0 * y).astype(z.dtype)


def merge_branches(ys, gate_logits, w_branch, w_out):
    y = jnp.stack(ys, axis=2)
    branch = jnp.einsum('blnc,ncd->blnd', y, w_branch)
    gates = jax.nn.sigmoid(gate_logits.astype(jnp.float32)).astype(branch.dtype).reshape(branch.shape)
    return jnp.sum(gates * branch, axis=2) @ w_out


def token_mixer(h, hc, w_in, ret_log_decay, ret_gn_g,
                s5_lambda_re, s5_lambda_im, s5_log_dt, s5_b_re, s5_b_im, s5_c_re, s5_c_im, s5_d,
                s5_glu_w1, s5_glu_w2, hy_short_w, hy_short_b, hy_f_w1, hy_f_b1, hy_f_w2, hy_f_b2,
                hy_f_w3, hy_f_freq, hy_bias, w_branch, w_out, need_ctx):
    z = h @ w_in
    if need_ctx:
        zc = hc @ w_in
        kvc, uc = zc[..., COL_K:COL_G], zc[..., COL_S5:COL_HY]
    else:
        kvc = hc @ w_in[:, COL_K:COL_G]
        uc = hc @ w_in[:, COL_S5:COL_HY]
    kc, vc = _heads(kvc[..., :RET_QK_W]), _heads(kvc[..., RET_QK_W:])
    q_scale = RET_DK ** -0.5

    s_f = retention_state(kc, vc, ret_log_decay[0])
    s_b = retention_state(jnp.flip(kc, axis=1), jnp.flip(vc, axis=1), ret_log_decay[1])
    q = _heads(z[..., COL_Q:COL_K]) * q_scale
    o = bi_retention(q, _heads(z[..., COL_K:COL_V]), _heads(z[..., COL_V:COL_G]), ret_log_decay, s_f, s_b)
    y_ret = retention_readout(o, z[..., COL_G:COL_S5], ret_gn_g)

    y_s5, y_s5c = s5_mixer(z[..., COL_S5:COL_HY], uc, s5_lambda_re, s5_lambda_im, s5_log_dt,
                           s5_b_re, s5_b_im, s5_c_re, s5_c_im, s5_d, s5_glu_w1, s5_glu_w2, need_ctx)

    filt_p = (hy_f_w1, hy_f_b1, hy_f_w2, hy_f_b2, hy_f_w3, hy_f_freq)
    y_hy = hyena_seq(z[..., COL_HY:COL_GATE], hy_short_w, hy_short_b, hyena_filter(z.shape[1], *filt_p), hy_bias)

    out = merge_branches((y_ret, y_s5, y_hy), z[..., COL_GATE:], w_branch, w_out)
    if not need_ctx:
        return out, None

    zero_s = jnp.zeros_like(s_f)
    qc = _heads(zc[..., COL_Q:COL_K]) * q_scale
    y_retc = retention_readout(bi_retention(qc, kc, vc, ret_log_decay, zero_s, zero_s), zc[..., COL_G:COL_S5], ret_gn_g)
    y_hyc = hyena_seq(zc[..., COL_HY:COL_GATE], hy_short_w, hy_short_b, hyena_filter(zc.shape[1], *filt_p), hy_bias)
    out_c = merge_branches((y_retc, y_s5c, y_hyc), zc[..., COL_GATE:], w_branch, w_out)
    return out, out_c


def grouped_expert_ffn(t, expert_ids, weights, w1, w3, w2):
    n_tok, d = t.shape
    n_slots = expert_ids.shape[0]
    k = n_slots // n_tok
    token = jnp.arange(n_slots) // k
    order = jnp.argsort(expert_ids)
    e_sorted, tok_sorted, w_sorted = expert_ids[order], token[order], weights[order]
    counts = jnp.bincount(expert_ids, length=MOE_EXPERTS)
    padded = (counts + MOE_BLOCK - 1) // MOE_BLOCK * MOE_BLOCK
    starts = jnp.cumsum(counts) - counts
    pends = jnp.cumsum(padded)
    pstarts = pends - padded
    dest = pstarts[e_sorted] + jnp.arange(n_slots) - starts[e_sorted]
    n_blocks = -(-n_slots // MOE_BLOCK) + MOE_EXPERTS
    buf = jnp.zeros((n_blocks * MOE_BLOCK, d), t.dtype).at[dest].set(t[tok_sorted])
    block_expert = jnp.minimum(jnp.searchsorted(pends, jnp.arange(n_blocks) * MOE_BLOCK, side='right'),
                               MOE_EXPERTS - 1)

    def expert_block(args):
        xb, e = args
        return (jax.nn.silu(xb @ w1[e]) * (xb @ w3[e])) @ w2[e]

    ybuf = lax.map(expert_block, (buf.reshape(n_blocks, MOE_BLOCK, d), block_expert)).reshape(-1, d)
    out = jnp.zeros((n_tok, d), jnp.float32).at[tok_sorted].add(ybuf[dest].astype(jnp.float32) * w_sorted[:, None])
    return out.astype(t.dtype)


def hier_moe(t, w_group, b_group, w_expert, b_expert, w1, w3, w2):
    n_tok = t.shape[0]
    tf = t.astype(jnp.float32)
    p_group = jax.nn.softmax(tf @ w_group.astype(jnp.float32) + b_group.astype(jnp.float32), axis=-1)
    p_top, g_idx = lax.top_k(p_group, 1)
    logits = (tf @ w_expert.astype(jnp.float32) + b_expert.astype(jnp.float32)).reshape(n_tok, MOE_GROUPS, MOE_PER_GROUP)
    in_group = jnp.take_along_axis(logits, g_idx[:, :, None], axis=1)[:, 0]
    l_top, e_idx = lax.top_k(in_group, MOE_TOPK)
    weights = jax.nn.softmax(l_top, axis=-1) * p_top
    experts = g_idx * MOE_PER_GROUP + e_idx
    return grouped_expert_ffn(t, experts.reshape(-1), weights.reshape(-1), w1, w3, w2)


def setup_inputs(seed: int = 0) -> dict:
    key = jax.random.key(seed)
    keys = iter(jax.random.split(key, 64))
    f32 = jnp.float32

    def normal(shape, scale):
        return jax.random.normal(next(keys), shape, f32) * scale

    d = D_MODEL
    g, n, p = S5_GROUPS, S5_STATE, S5_GROUP
    ret_base = jnp.log1p(-(2.0 ** (-5.0 - jnp.arange(RET_HEADS, dtype=f32))))
    return {
        'x': normal((BATCH, SEQ, d), 1.0),
        'c': normal((BATCH, d), 1.0),
        'ctx': normal((BATCH, CTX_LEN, d), 1.0),
        'c_ctx': normal((d,), 1.0),
        'w_mod': normal((DEPTH, d, 6 * d), 0.5 * d ** -0.5),
        'b_mod': normal((DEPTH, 6 * d), 0.01),
        'norm1_g': 1.0 + normal((DEPTH, d), 0.01),
        'norm2_g': 1.0 + normal((DEPTH, d), 0.01),
        'w_in': normal((DEPTH, d, IN_COLS), d ** -0.5),
        'ret_log_decay': ret_base * (1.0 + normal((DEPTH, 2, RET_HEADS), 0.05)),
        'ret_gn_g': 1.0 + normal((DEPTH, MIX_W), 0.01),
        's5_lambda_re': -0.5 + normal((DEPTH, 2, g, n), 0.01),
        's5_lambda_im': math.pi * jnp.arange(n, dtype=f32) + normal((DEPTH, 2, g, n), 0.01),
        's5_log_dt': jax.random.uniform(next(keys), (DEPTH, 2, g), f32, math.log(S5_DT_MIN), math.log(S5_DT_MAX)),
        's5_b_re': normal((DEPTH, 2, g, n, p), (2 * p) ** -0.5),
        's5_b_im': normal((DEPTH, 2, g, n, p), (2 * p) ** -0.5),
        's5_c_re': normal((DEPTH, 2, g, p, n), n ** -0.5),
        's5_c_im': normal((DEPTH, 2, g, p, n), n ** -0.5),
        's5_d': normal((DEPTH, MIX_W), 1.0),
        's5_glu_w1': normal((DEPTH, MIX_W, MIX_W), MIX_W ** -0.5),
        's5_glu_w2': normal((DEPTH, MIX_W, MIX_W), MIX_W ** -0.5),
        'hy_short_w': normal((DEPTH, HY_SHORT, 3 * MIX_W), 0.5),
        'hy_short_b': normal((DEPTH, 3 * MIX_W), 0.01),
        'hy_f_w1': normal((DEPTH, HY_EMB, HY_FHID), HY_EMB ** -0.5),
        'hy_f_b1': normal((DEPTH, HY_FHID), 0.1),
        'hy_f_w2': normal((DEPTH, HY_FHID, HY_FHID), HY_FHID ** -0.5),
        'hy_f_b2': normal((DEPTH, HY_FHID), 0.1),
        'hy_f_w3': normal((DEPTH, HY_FHID, MIX_W), 0.05 * HY_FHID ** -0.5),
        'hy_f_freq': 1.0 + normal((DEPTH, 2, HY_FHID), 0.01),
        'hy_bias': normal((DEPTH, MIX_W), 0.5),
        'w_branch': normal((DEPTH, N_BRANCH, MIX_W, d), MIX_W ** -0.5),
        'w_out': normal((DEPTH, d, d), d ** -0.5),
        'moe_w_group': normal((DEPTH, d, MOE_GROUPS), d ** -0.5),
        'moe_b_group': normal((DEPTH, MOE_GROUPS), 0.01),
        'moe_w_expert': normal((DEPTH, d, MOE_EXPERTS), d ** -0.5),
        'moe_b_expert': normal((DEPTH, MOE_EXPERTS), 0.01),
        'moe_w1': normal((DEPTH, MOE_EXPERTS, d, MOE_HIDDEN), d ** -0.5),
        'moe_w3': normal((DEPTH, MOE_EXPERTS, d, MOE_HIDDEN), d ** -0.5),
        'moe_w2': normal((DEPTH, MOE_EXPERTS, MOE_HIDDEN, d), MOE_HIDDEN ** -0.5),
        'final_norm_g': 1.0 + normal((d,), 0.01),
    }


def reference(x, c, ctx, c_ctx, w_mod, b_mod, norm1_g, norm2_g, w_in, ret_log_decay, ret_gn_g,
              s5_lambda_re, s5_lambda_im, s5_log_dt, s5_b_re, s5_b_im, s5_c_re, s5_c_im, s5_d,
              s5_glu_w1, s5_glu_w2, hy_short_w, hy_short_b, hy_f_w1, hy_f_b1, hy_f_w2, hy_f_b2,
              hy_f_w3, hy_f_freq, hy_bias, w_branch, w_out, moe_w_group, moe_b_group,
              moe_w_expert, moe_b_expert, moe_w1, moe_w3, moe_w2, final_norm_g):
    bsz, n_lat, d = x.shape
    x = x + grid_pos_embed(n_lat, x.dtype)[None]
    xc = ctx
    silu_c = jax.nn.silu(c)
    silu_cc = jax.nn.silu(c_ctx)
    for l in range(DEPTH):
        last = l == DEPTH - 1
        mod = (silu_c @ w_mod[l] + b_mod[l])[:, None, :]
        sh1, sc1, g1, sh2, sc2, g2 = jnp.split(mod, 6, axis=-1)
        n_ctx_mod = 2 if last else 6
        modc = (silu_cc @ w_mod[l][:, :n_ctx_mod * d] + b_mod[l][:n_ctx_mod * d])[None, None, :]
        cm = jnp.split(modc, n_ctx_mod, axis=-1)
        h = rms_norm(x, norm1_g[l]) * (1 + sc1) + sh1
        hc = rms_norm(xc, norm1_g[l]) * (1 + cm[1]) + cm[0]
        mix, mix_c = token_mixer(h, hc, w_in[l], ret_log_decay[l], ret_gn_g[l],
                                 s5_lambda_re[l], s5_lambda_im[l], s5_log_dt[l], s5_b_re[l], s5_b_im[l],
                                 s5_c_re[l], s5_c_im[l], s5_d[l], s5_glu_w1[l], s5_glu_w2[l],
                                 hy_short_w[l], hy_short_b[l], hy_f_w1[l], hy_f_b1[l], hy_f_w2[l], hy_f_b2[l],
                                 hy_f_w3[l], hy_f_freq[l], hy_bias[l], w_branch[l], w_out[l],
                                 need_ctx=not last)
        x = x + g1 * mix
        h2 = rms_norm(x, norm2_g[l]) * (1 + sc2) + sh2
        moe_p = (moe_w_group[l], moe_b_group[l], moe_w_expert[l], moe_b_expert[l], moe_w1[l], moe_w3[l], moe_w2[l])
        if last:
            x = x + g2 * hier_moe(h2.reshape(-1, d), *moe_p).reshape(x.shape)
        else:
            xc = xc + cm[2] * mix_c
            h2c = rms_norm(xc, norm2_g[l]) * (1 + cm[4]) + cm[3]
            y = hier_moe(jnp.concatenate([h2.reshape(-1, d), h2c.reshape(-1, d)], axis=0), *moe_p)
            x = x + g2 * y[: bsz * n_lat].reshape(x.shape)
            xc = xc + cm[5] * y[bsz * n_lat:].reshape(xc.shape)
    return rms_norm(x, final_norm_g)
```

```python
import functools
import math

import jax
import jax.numpy as jnp
from jax import lax
from jax.experimental import pallas as pl
from jax.experimental.pallas import tpu as pltpu

F32 = jnp.float32
BF16 = jnp.bfloat16
HI = lax.Precision.HIGHEST

D_MODEL = 1024
GRID_W = 64
RMS_EPS = 1e-6
GN_EPS = 1e-5
MIX_W = 512
RET_HEADS = 4
RET_DK = 128
S5_GROUP = 16
S5_GROUPS = MIX_W // S5_GROUP
S5_STATE = 64
HY_EMB = 33
HY_BANDS = (HY_EMB - 1) // 2
HY_DECAY_TARGET = 1e-2
HY_FAST_PCT = 0.3
HY_SLOW_PCT = 1.5
MOE_GROUPS = 4
MOE_PER_GROUP = 8
MOE_EXPERTS = MOE_GROUPS * MOE_PER_GROUP
MOE_HIDDEN = 512
MOE_TOPK = 2

COL_Q = 0
COL_K = COL_Q + RET_HEADS * RET_DK
COL_V = COL_K + RET_HEADS * RET_DK
COL_G = COL_V + MIX_W
COL_S5 = COL_G + MIX_W
COL_HY = COL_S5 + MIX_W
COL_GATE = COL_HY + 3 * MIX_W
IN_COLS = COL_GATE + 3 * D_MODEL

SUBLANES = 8
LANES = 128
VMEM_LIMIT = 56 * 1024 * 1024

RET_CHUNK = 256
S5_CHUNK = 32
S5_ROWS = 8
HY_FBLK = 128
HY_CT = 256
MOE_BM = 256
MOE_TC = 128
ROUTE_COLS = 128


def _cparams(sem, vmem=VMEM_LIMIT):
    return pltpu.CompilerParams(dimension_semantics=sem, vmem_limit_bytes=vmem)


def _proj_body(x_ref, mod_ref, g_ref, w_ref, z_ref, h_scr):
    @pl.when(pl.program_id(1) == 0)
    def _():
        x = x_ref[...]
        y = x * lax.rsqrt(jnp.mean(x * x, axis=-1, keepdims=True) + RMS_EPS) * g_ref[...]
        h_scr[...] = (y * (1.0 + mod_ref[0, 1:2, :]) + mod_ref[0, 0:1, :]).astype(BF16)

    z_ref[...] = jnp.dot(h_scr[...], w_ref[...], preferred_element_type=F32)


def _proj(x, mod, g, w, rows_per_seq):
    t, d = x.shape
    n = w.shape[1]
    tm = min(1024, rows_per_seq)
    tn = 1024
    tiles_per_seq = rows_per_seq // tm
    return pl.pallas_call(
        _proj_body,
        grid=(t // tm, n // tn),
        in_specs=[
            pl.BlockSpec((tm, d), lambda i, j: (i, 0)),
            pl.BlockSpec((1, 8, d), lambda i, j: (i // tiles_per_seq, 0, 0)),
            pl.BlockSpec((1, d), lambda i, j: (0, 0)),
            pl.BlockSpec((d, tn), lambda i, j: (0, j)),
        ],
        out_specs=pl.BlockSpec((tm, tn), lambda i, j: (i, j)),
        out_shape=jax.ShapeDtypeStruct((t, n), F32),
        scratch_shapes=[pltpu.VMEM((tm, d), BF16)],
        compiler_params=_cparams(("arbitrary", "arbitrary")),
    )(x, mod, g, w)


def _ret_tables(log_decay, c):
    lgf = log_decay[0].astype(F32)[:, None]
    lgb = log_decay[1].astype(F32)[:, None]
    idx = jnp.arange(c, dtype=F32)[None, :]
    scale = RET_DK ** -0.5
    diff = idx[0][:, None] - idx[0][None, :]
    mask = jnp.where(diff[None] >= 0,
                     jnp.exp(lgf[:, :, None] * jnp.maximum(diff, 0.0)[None]),
                     jnp.exp(lgb[:, :, None] * jnp.maximum(-diff, 0.0)[None])) * scale
    qdf = jnp.exp(lgf * (idx + 1.0)) * scale
    kdf = jnp.exp(lgf * (c - 1.0 - idx))
    qdb = jnp.exp(lgb * (c - idx)) * scale
    kdb = jnp.exp(lgb * idx)
    dec = jnp.stack([qdf, kdf, qdb, kdb], axis=1)
    dec = jnp.broadcast_to(dec[..., None], dec.shape + (LANES,))
    cdec = jnp.stack([jnp.exp(lgf * c), jnp.exp(lgb * c)], axis=1)
    cdec = jnp.broadcast_to(cdec[..., None], (cdec.shape[0], 2, SUBLANES, LANES))
    return mask, dec, cdec


def _ret_body(q_ref, k_ref, v_ref, g_ref, mask_ref, dec_ref, cdec_ref, gn_ref, s0f_ref, s0b_ref,
              y_ref, sff_ref, sfb_ref, sb_scr, *, n_chunks, c):
    mask = mask_ref[0]
    qdf, kdf, qdb, kdb = dec_ref[0, 0], dec_ref[0, 1], dec_ref[0, 2], dec_ref[0, 3]
    cf = cdec_ref[0, 0, 0:1, :]
    cb = cdec_ref[0, 1, 0:1, :]
    gn = gn_ref[...]

    def rows(ci):
        return pl.ds(pl.multiple_of(ci * c, c), c)

    def kv_state(k, v, kdec):
        kt = jnp.transpose(k * kdec).astype(BF16)
        return jnp.dot(kt, v.astype(BF16), preferred_element_type=F32)

    def bwd(i, sb):
        ci = n_chunks - 1 - i
        sb_scr[ci] = sb
        return sb * cb + kv_state(k_ref[rows(ci), :], v_ref[rows(ci), :], kdb)

    sfb_ref[0, 0] = lax.fori_loop(0, n_chunks, bwd, s0b_ref[0, 0])

    def fwd(ci, sf):
        q = q_ref[rows(ci), :]
        k = k_ref[rows(ci), :]
        v = v_ref[rows(ci), :]
        att = lax.dot_general(q.astype(BF16), k.astype(BF16), (((1,), (1,)), ((), ())),
                              preferred_element_type=F32) * mask
        o = jnp.dot(att.astype(BF16), v.astype(BF16), preferred_element_type=F32)
        o += jnp.dot((q * qdf).astype(BF16), sf.astype(BF16), preferred_element_type=F32)
        o += jnp.dot((q * qdb).astype(BF16), sb_scr[ci].astype(BF16), preferred_element_type=F32)
        mu = jnp.mean(o, axis=-1, keepdims=True)
        oc = o - mu
        var = jnp.mean(oc * oc, axis=-1, keepdims=True)
        g = g_ref[rows(ci), :]
        y_ref[rows(ci), :] = oc * lax.rsqrt(var + GN_EPS) * gn * (g * jax.nn.sigmoid(g))
        return sf * cf + kv_state(k, v, kdf)

    sff_ref[0, 0] = lax.fori_loop(0, n_chunks, fwd, s0f_ref[0, 0])


def _retention(z, n_seq, seq_len, tables, gn_g, s0f, s0b):
    mask, dec, cdec = tables
    c = mask.shape[-1]
    h, dk = RET_HEADS, RET_DK
    colblk = lambda base: (lambda s, hh: (s, base // dk + hh))
    st_spec = pl.BlockSpec((1, 1, dk, dk), lambda s, hh: (s, hh, 0, 0))
    body = functools.partial(_ret_body, n_chunks=seq_len // c, c=c)
    return pl.pallas_call(
        body,
        grid=(n_seq, h),
        in_specs=[
            pl.BlockSpec((seq_len, dk), colblk(COL_Q)),
            pl.BlockSpec((seq_len, dk), colblk(COL_K)),
            pl.BlockSpec((seq_len, dk), colblk(COL_V)),
            pl.BlockSpec((seq_len, dk), colblk(COL_G)),
            pl.BlockSpec((1, c, c), lambda s, hh: (hh, 0, 0)),
            pl.BlockSpec((1, 4, c, LANES), lambda s, hh: (hh, 0, 0, 0)),
            pl.BlockSpec((1, 2, SUBLANES, LANES), lambda s, hh: (hh, 0, 0, 0)),
            pl.BlockSpec((1, dk), lambda s, hh: (0, hh)),
            st_spec, st_spec,
        ],
        out_specs=[pl.BlockSpec((seq_len, dk), lambda s, hh: (s, hh)), st_spec, st_spec],
        out_shape=[jax.ShapeDtypeStruct((n_seq * seq_len, MIX_W), F32),
                   jax.ShapeDtypeStruct((n_seq, h, dk, dk), F32),
                   jax.ShapeDtypeStruct((n_seq, h, dk, dk), F32)],
        scratch_shapes=[pltpu.VMEM((seq_len // c, dk, dk), F32)],
        compiler_params=_cparams(("arbitrary", "arbitrary")),
    )(z, z, z, z, mask, dec, cdec, gn_g, s0f, s0b)


def _s5_operators(lam_re, lam_im, log_dt, b_re, b_im, c_re, c_im):
    tc, p, n, g = S5_CHUNK, S5_GROUP, S5_STATE, S5_GROUPS
    taus = jnp.arange(tc + 1, dtype=F32)
    ks, ins, outs, ats = [], [], [], []
    for d in range(2):
        lam = lax.complex(lam_re[d].astype(F32), lam_im[d].astype(F32))
        lam_dt = lam * jnp.exp(log_dt[d].astype(F32))[:, None]
        a_bar = jnp.exp(lam_dt)
        bmat = lax.complex(b_re[d].astype(F32), b_im[d].astype(F32))
        b_bar = ((a_bar - 1.0) / lam)[..., None] * bmat
        cmat = lax.complex(c_re[d].astype(F32), c_im[d].astype(F32))
        pw = jnp.exp(taus[:, None, None] * lam_dt[None])
        ks.append(jnp.einsum('gpn,tgn,gnq->tgpq', cmat, pw[:tc], b_bar, precision=HI).real)
        pin = pw[:tc][::-1] if d == 0 else pw[:tc]
        ins.append(jnp.einsum('sgn,gnq->gsqn', pin, b_bar))
        pout = pw[1:] if d == 0 else pw[1:][::-1]
        outs.append(jnp.einsum('gpn,tgn->gntp', cmat, pout))
        ats.append(pw[tc])
    kf, kb = ks
    lag = jnp.arange(tc)[None, :] - jnp.arange(tc)[:, None]
    kfull = jnp.where((lag >= 0)[:, :, None, None, None], kf[jnp.maximum(lag, 0)], 0.0) \
        + jnp.where((lag <= 0)[:, :, None, None, None], kb[jnp.maximum(-lag, 0)], 0.0)
    tmix = kfull.transpose(2, 0, 4, 1, 3).reshape(g, tc * p, tc * p)
    win = jnp.concatenate([ins[0].real, ins[1].real, ins[0].imag, ins[1].imag], axis=-1)
    win = win.reshape(g, tc * p, 4 * n)
    wout = jnp.concatenate([outs[0].real, outs[1].real, -outs[0].imag, -outs[1].imag], axis=1)
    wout = wout.reshape(g, 4 * n, tc * p)
    at = jnp.stack([jnp.concatenate([ats[0].real, ats[1].real], axis=-1),
                    jnp.concatenate([ats[0].imag, ats[1].imag], axis=-1)], axis=1)
    at = jnp.broadcast_to(at[:, :, None, :], (g, 2, S5_ROWS, 2 * n))
    return tmix.astype(BF16), win.astype(BF16), wout.astype(BF16), at


def _s5_body(uc_ref, ul_ref, tmix_ref, win_ref, wout_ref, at_ref, yc_ref, yl_ref,
             inc_scr, inl_scr, hac_scr, hbc_scr, hal_scr, hbl_scr, *, ncc, ncl):
    n2 = 2 * S5_STATE
    win = win_ref[0]
    inc_scr[...] = jnp.dot(uc_ref[0], win, preferred_element_type=F32)
    inl_scr[...] = jnp.dot(ul_ref[0], win, preferred_element_type=F32)
    ar = at_ref[0, 0]
    ai = at_ref[0, 1]
    is_fwd = lax.broadcasted_iota(jnp.int32, (S5_ROWS, n2), 1) < S5_STATE

    def make_step(in_scr, ha_scr, hb_scr, nc):
        def step(i, carry):
            hr, hi = carry
            ri = pl.ds(pl.multiple_of(i * S5_ROWS, S5_ROWS), S5_ROWS)
            rj = pl.ds(pl.multiple_of((nc - 1 - i) * S5_ROWS, S5_ROWS), S5_ROWS)
            ha_scr[ri, 0:n2] = hr
            ha_scr[ri, n2:2 * n2] = hi
            hb_scr[rj, 0:n2] = hr
            hb_scr[rj, n2:2 * n2] = hi
            a = in_scr[ri, :]
            b = in_scr[rj, :]
            inr = jnp.where(is_fwd, a[:, 0:n2], b[:, 0:n2])
            ini = jnp.where(is_fwd, a[:, n2:2 * n2], b[:, n2:2 * n2])
            return ar * hr - ai * hi + inr, ar * hi + ai * hr + ini
        return step

    zero = jnp.zeros((S5_ROWS, n2), F32)
    carry = lax.fori_loop(0, ncc, make_step(inc_scr, hac_scr, hbc_scr, ncc), (zero, zero))
    lax.fori_loop(0, ncl, make_step(inl_scr, hal_scr, hbl_scr, ncl), carry)

    sel = (lax.broadcasted_iota(jnp.int32, (1, 2 * n2), 1) % n2) < S5_STATE
    tmix = tmix_ref[0]
    wout = wout_ref[0]
    hs_c = jnp.where(sel, hac_scr[...], hbc_scr[...]).astype(BF16)
    yc_ref[0] = (jnp.dot(uc_ref[0], tmix, preferred_element_type=F32)
                 + jnp.dot(hs_c, wout, preferred_element_type=F32))
    hs_l = jnp.where(sel, hal_scr[...], hbl_scr[...]).astype(BF16)
    yl_ref[0] = (jnp.dot(ul_ref[0], tmix, preferred_element_type=F32)
                 + jnp.dot(hs_l, wout, preferred_element_type=F32))


def _s5_to_groups(u, n_seq, seq_len):
    tc, p, g = S5_CHUNK, S5_GROUP, S5_GROUPS
    nc = seq_len // tc
    ug = u.astype(BF16).reshape(n_seq, nc, tc, g, p).transpose(3, 1, 0, 2, 4)
    ug = jnp.pad(ug, ((0, 0), (0, 0), (0, S5_ROWS - n_seq), (0, 0), (0, 0)))
    return ug.reshape(g, nc * S5_ROWS, tc * p)


def _s5_from_groups(yg, n_seq, seq_len):
    tc, p, g = S5_CHUNK, S5_GROUP, S5_GROUPS
    nc = seq_len // tc
    y = yg.reshape(g, nc, S5_ROWS, tc, p)[:, :, :n_seq].transpose(2, 1, 3, 0, 4)
    return y.reshape(n_seq * seq_len, MIX_W)


def _s5(uc_g, ul_g, ops):
    tmix, win, wout, at = ops
    g, rc, w = uc_g.shape
    rl = ul_g.shape[1]
    n4 = 4 * S5_STATE
    body = functools.partial(_s5_body, ncc=rc // S5_ROWS, ncl=rl // S5_ROWS)
    gspec = lambda r, cdim: pl.BlockSpec((1, r, cdim), lambda i: (i, 0, 0))
    return pl.pallas_call(
        body,
        grid=(g,),
        in_specs=[gspec(rc, w), gspec(rl, w), gspec(w, w), gspec(w, n4), gspec(n4, w),
                  pl.BlockSpec((1, 2, S5_ROWS, 2 * S5_STATE), lambda i: (i, 0, 0, 0))],
        out_specs=[gspec(rc, w), gspec(rl, w)],
        out_shape=[jax.ShapeDtypeStruct((g, rc, w), F32), jax.ShapeDtypeStruct((g, rl, w), F32)],
        scratch_shapes=[pltpu.VMEM((rc, n4), F32), pltpu.VMEM((rl, n4), F32),
                        pltpu.VMEM((rc, n4), F32), pltpu.VMEM((rc, n4), F32),
                        pltpu.VMEM((rl, n4), F32), pltpu.VMEM((rl, n4), F32)],
        compiler_params=_cparams(("arbitrary",)),
    )(uc_g, ul_g, tmix, win, wout, at)


def _hyena_filter(n_tok, w1, b1, w2, b2, w3, freq):
    t = jnp.linspace(0.0, 1.0, n_tok, dtype=F32)[:, None]
    w = 2.0 * math.pi * jnp.arange(n_tok, dtype=F32)[:, None] / n_tok
    f = jnp.linspace(1e-4, HY_BANDS - 1, HY_BANDS, dtype=F32)[None]
    z = jnp.concatenate([t, jnp.cos(f * w), -jnp.sin(f * w)], axis=-1)
    freq = freq.astype(F32)
    h = jnp.sin(freq[0] * (jnp.dot(z, w1.astype(F32), precision=HI) + b1.astype(F32)))
    h = jnp.sin(freq[1] * (jnp.dot(h, w2.astype(F32), precision=HI) + b2.astype(F32)))
    h = jnp.dot(h, w3.astype(F32), precision=HI)
    offset = jnp.abs(jnp.arange(n_tok) - n_tok // 2).astype(F32) / n_tok
    deltas = jnp.abs(jnp.linspace(math.log(HY_DECAY_TARGET) / HY_SLOW_PCT,
                                  math.log(HY_DECAY_TARGET) / HY_FAST_PCT, MIX_W, dtype=F32))
    return h * jnp.exp(-offset[:, None] * deltas[None])


def _cis(mult, n_fft):
    ang = (mult % (2 * n_fft)).astype(F32) * (math.pi / n_fft)
    return jnp.cos(ang), jnp.sin(ang)


def _dft_matrices(n):
    n_fft = 2 * n
    fb = min(HY_FBLK, n)
    odd = 2 * jnp.arange(n, dtype=jnp.int32)[:, None] + 1
    t_hi = 64 * jnp.arange(n // 64, dtype=jnp.int32)[None, :]
    t_lo = jnp.arange(64, dtype=jnp.int32)[None, :]
    bc, bs = _cis(odd * t_lo, n_fft)

    def table(ac, as_):
        c = (ac[:, :, None] * bc[:, None, :] - as_[:, :, None] * bs[:, None, :]).reshape(n, n)
        s = (as_[:, :, None] * bc[:, None, :] + ac[:, :, None] * bs[:, None, :]).reshape(n, n)
        return c, s

    fc, fs = table(*_cis(odd * t_hi, n_fft))
    fwd = jnp.concatenate([fc.reshape(n // fb, fb, n), -fs.reshape(n // fb, fb, n)], axis=1)
    gc, gs = table(*_cis(odd * (t_hi + n // 2), n_fft))
    inv = jnp.concatenate([gc.reshape(n // fb, fb, n), -gs.reshape(n // fb, fb, n)], axis=1)
    inv = (inv * (2.0 / n_fft)).transpose(0, 2, 1)
    return fwd.astype(BF16), inv.astype(BF16)


def _filter_spectrum(filt):
    n = filt.shape[0]
    fb = min(HY_FBLK, n)
    j = jnp.arange(n, dtype=F32)
    ang = -math.pi * j / (2 * n)
    mod = lax.complex(jnp.cos(ang), jnp.sin(ang))[:, None]
    spec = jnp.fft.fft(filt.astype(jnp.complex64) * mod, n=2 * n, axis=0)[:n]
    re = spec.real.reshape(n // fb, fb, -1)
    im = spec.imag.reshape(n // fb, fb, -1)
    return jnp.concatenate([re, im], axis=1).astype(F32)


def _short_conv(z, w):
    n = z.shape[0]
    row = lax.broadcasted_iota(jnp.int32, z.shape, 0)
    prev = jnp.where(row == 0, 0.0, pltpu.roll(z, 1, axis=0))
    nxt = jnp.where(row == n - 1, 0.0, pltpu.roll(z, n - 1, axis=0))
    return prev * w[0:1, :] + z * w[1:2, :] + nxt * w[2:3, :] + w[3:4, :]


def _hy_pre_body(x0_ref, x1_ref, v_ref, w0_ref, w1_ref, w2_ref, x0c_ref, yv_ref):
    x0c_ref[...] = _short_conv(x0_ref[...], w0_ref[...])
    yv_ref[...] = _short_conv(x1_ref[...], w1_ref[...]) * _short_conv(v_ref[...], w2_ref[...])


def _hy_pre(z, n_seq, seq_len, sw):
    ct = HY_CT
    nct = MIX_W // ct
    zspec = lambda base: pl.BlockSpec((seq_len, ct), lambda s, j: (s, base // ct + j))
    wspec = lambda base: pl.BlockSpec((8, ct), lambda s, j: (0, base // ct + j))
    ospec = pl.BlockSpec((seq_len, ct), lambda s, j: (s, j))
    oshape = jax.ShapeDtypeStruct((n_seq * seq_len, MIX_W), F32)
    return pl.pallas_call(
        _hy_pre_body,
        grid=(n_seq, nct),
        in_specs=[zspec(COL_HY), zspec(COL_HY + MIX_W), zspec(COL_HY + 2 * MIX_W),
                  wspec(0), wspec(MIX_W), wspec(2 * MIX_W)],
        out_specs=[ospec, ospec],
        out_shape=[oshape, oshape],
        compiler_params=_cparams(("arbitrary", "arbitrary")),
    )(z, z, z, sw, sw, sw)


def _hy_dft_body(yv_ref, x0_ref, f_ref, g_ref, h_ref, bias_ref, o_ref, yb_scr, acc_scr, *, fb):
    m = pl.program_id(2)

    @pl.when(m == 0)
    def _():
        yb_scr[...] = yv_ref[...].astype(BF16)
        acc_scr[...] = jnp.zeros_like(acc_scr)

    u = jnp.dot(f_ref[0], yb_scr[...], preferred_element_type=F32)
    ur, ui = u[:fb], u[fb:]
    hr, hi = h_ref[0, :fb, :], h_ref[0, fb:, :]
    zz = jnp.concatenate([ur * hr - ui * hi, ur * hi + ui * hr], axis=0).astype(BF16)
    acc_scr[...] += jnp.dot(g_ref[0], zz, preferred_element_type=F32)

    @pl.when(m == pl.num_programs(2) - 1)
    def _():
        o_ref[...] = x0_ref[...] * (acc_scr[...] + yv_ref[...] * bias_ref[...])


def _hy_dft(yv, x0c, mats, spec, bias, n_seq, seq_len):
    fwd, inv = mats
    nblk, fb2, _ = fwd.shape
    ct = HY_CT
    body = functools.partial(_hy_dft_body, fb=fb2 // 2)
    tspec = pl.BlockSpec((seq_len, ct), lambda s, j, m: (s, j))
    return pl.pallas_call(
        body,
        grid=(n_seq, MIX_W // ct, nblk),
        in_specs=[tspec, tspec,
                  pl.BlockSpec((1, fb2, seq_len), lambda s, j, m: (m, 0, 0)),
                  pl.BlockSpec((1, seq_len, fb2), lambda s, j, m: (m, 0, 0)),
                  pl.BlockSpec((1, fb2, ct), lambda s, j, m: (m, 0, j)),
                  pl.BlockSpec((1, ct), lambda s, j, m: (0, j))],
        out_specs=tspec,
        out_shape=jax.ShapeDtypeStruct((n_seq * seq_len, MIX_W), F32),
        scratch_shapes=[pltpu.VMEM((seq_len, ct), BF16), pltpu.VMEM((seq_len, ct), F32)],
        compiler_params=_cparams(("arbitrary", "arbitrary", "arbitrary")),
    )(yv, x0c, fwd, inv, spec, bias)


def _merge_body(yret_ref, ys5_ref, u_ref, yhy_ref, g0_ref, g1_ref, g2_ref, x_ref, mod1_ref, mod2_ref,
                d_ref, gw1_ref, gw2_ref, wb_ref, wo_ref, n2g_ref, wr_ref, br_ref,
                xo_ref, h2_ref, lg_ref):
    yy = ys5_ref[...] + d_ref[...] * u_ref[...]
    ge = jax.nn.gelu(yy).astype(BF16)
    s5o = (jnp.dot(ge, gw1_ref[...], preferred_element_type=F32)
           * jax.nn.sigmoid(jnp.dot(ge, gw2_ref[...], preferred_element_type=F32)))
    m = jax.nn.sigmoid(g0_ref[...]) * jnp.dot(yret_ref[...].astype(BF16), wb_ref[0],
                                              preferred_element_type=F32)
    m += jax.nn.sigmoid(g1_ref[...]) * jnp.dot(s5o.astype(BF16), wb_ref[1], preferred_element_type=F32)
    m += jax.nn.sigmoid(g2_ref[...]) * jnp.dot(yhy_ref[...].astype(BF16), wb_ref[2],
                                               preferred_element_type=F32)
    out = jnp.dot(m.astype(BF16), wo_ref[...], preferred_element_type=F32)
    x = x_ref[...] + mod1_ref[0, 2:3, :] * out
    xo_ref[...] = x
    y = x * lax.rsqrt(jnp.mean(x * x, axis=-1, keepdims=True) + RMS_EPS) * n2g_ref[...]
    h2 = y * (1.0 + mod2_ref[0, 1:2, :]) + mod2_ref[0, 0:1, :]
    h2_ref[...] = h2
    lg_ref[...] = jnp.dot(h2, wr_ref[...], preferred_element_type=F32, precision=HI) + br_ref[...]


def _merge(yret, ys5, z, yhy, x, mod1, mod2, s5_d, gw1, gw2, wb, wo, n2g, wr, br, rows_per_seq):
    t, d = x.shape
    tm = 256
    tiles_per_seq = rows_per_seq // tm
    mw = MIX_W
    rspec = lambda w, cb=0: pl.BlockSpec((tm, w), lambda i: (i, cb))
    full = lambda a: pl.BlockSpec(a.shape, lambda i: (0,) * a.ndim)
    mspec = pl.BlockSpec((1, 8, d), lambda i: (i // tiles_per_seq, 0, 0))
    return pl.pallas_call(
        _merge_body,
        grid=(t // tm,),
        in_specs=[rspec(mw), rspec(mw), rspec(mw, COL_S5 // mw), rspec(mw),
                  rspec(d, COL_GATE // d), rspec(d, COL_GATE // d + 1), rspec(d, COL_GATE // d + 2),
                  rspec(d), mspec, mspec,
                  full(s5_d), full(gw1), full(gw2), full(wb), full(wo), full(n2g), full(wr), full(br)],
        out_specs=[rspec(d), rspec(d), rspec(ROUTE_COLS)],
        out_shape=[jax.ShapeDtypeStruct((t, d), F32), jax.ShapeDtypeStruct((t, d), F32),
                   jax.ShapeDtypeStruct((t, ROUTE_COLS), F32)],
        compiler_params=_cparams(("arbitrary",)),
    )(yret, ys5, z, yhy, z, z, z, x, mod1, mod2, s5_d, gw1, gw2, wb, wo, n2g, wr, br)


def _route(logits, n_rows_cap):
    t = logits.shape[0]
    p_group = jax.nn.softmax(logits[:, :MOE_GROUPS], axis=-1)
    p_top, g_idx = lax.top_k(p_group, 1)
    el = logits[:, MOE_GROUPS:MOE_GROUPS + MOE_EXPERTS].reshape(t, MOE_GROUPS, MOE_PER_GROUP)
    in_group = jnp.take_along_axis(el, g_idx[:, :, None], axis=1)[:, 0]
    l_top, e_idx = lax.top_k(in_group, MOE_TOPK)
    weights = jax.nn.softmax(l_top, axis=-1) * p_top
    experts = (g_idx * MOE_PER_GROUP + e_idx).astype(jnp.int32)
    flat_e = experts.reshape(-1)
    onehot = (flat_e[:, None] == jnp.arange(MOE_EXPERTS, dtype=jnp.int32)[None, :]).astype(jnp.int32)
    rank = jnp.take_along_axis(jnp.cumsum(onehot, axis=0) - onehot, flat_e[:, None], axis=1)[:, 0]
    counts = jnp.sum(onehot, axis=0)
    padded = (counts + MOE_BM - 1) // MOE_BM * MOE_BM
    pends = jnp.cumsum(padded)
    pstarts = pends - padded
    dest = (pstarts[flat_e] + rank).astype(jnp.int32)
    token = jnp.arange(t * MOE_TOPK, dtype=jnp.int32) // MOE_TOPK
    src = jnp.zeros((n_rows_cap,), jnp.int32).at[dest].set(token)
    n_blocks = n_rows_cap // MOE_BM
    blk_start = jnp.arange(n_blocks, dtype=jnp.int32) * MOE_BM
    blk_expert = jnp.minimum(jnp.searchsorted(pends, blk_start, side='right'), MOE_EXPERTS - 1)
    n_valid = (pends[-1] // MOE_BM).astype(jnp.int32).reshape(1)
    return (blk_expert.astype(jnp.int32), n_valid, src, dest.reshape(t, MOE_TOPK), weights.astype(F32))


def _ffn_body(be_ref, nv_ref, src_ref, h2_hbm, w1_ref, w3_ref, w2_ref, y_ref, xbuf, sem):
    i = pl.program_id(0)
    nv = nv_ref[0]
    rows = MOE_BM * SUBLANES

    def gather(blk, slot, start):
        def body(r, carry):
            tok = src_ref[blk * MOE_BM + r]
            cp = pltpu.make_async_copy(
                h2_hbm.at[pl.ds(pl.multiple_of(tok * SUBLANES, SUBLANES), SUBLANES), :],
                xbuf.at[pl.ds(pl.multiple_of(slot * rows + r * SUBLANES, SUBLANES), SUBLANES), :],
                sem.at[slot])
            if start:
                cp.start()
            else:
                cp.wait()
            return carry
        lax.fori_loop(0, MOE_BM, body, 0)

    @pl.when(jnp.logical_and(i == 0, nv > 0))
    def _():
        gather(0, 0, True)

    @pl.when(i + 1 < nv)
    def _():
        gather(i + 1, (i + 1) % 2, True)

    @pl.when(i < nv)
    def _():
        slot = i % 2
        gather(i, slot, False)
        base = slot * rows
        x = jnp.concatenate(
            [xbuf[pl.ds(base + s, MOE_BM, stride=SUBLANES), :] for s in range(SUBLANES)], axis=1)
        xb = x.astype(BF16)
        h1 = jnp.dot(xb, w1_ref[0].astype(BF16), preferred_element_type=F32)
        h3 = jnp.dot(xb, w3_ref[0].astype(BF16), preferred_element_type=F32)
        hh = (h1 * jax.nn.sigmoid(h1) * h3).astype(BF16)
        y_ref[...] = jnp.dot(hh, w2_ref[0].astype(BF16), preferred_element_type=F32)

    @pl.when(i >= nv)
    def _():
        y_ref[...] = jnp.zeros_like(y_ref)


def _moe_ffn(blk_expert, n_valid, src, h2_tiles, w1, w3, w2):
    n_blocks = blk_expert.shape[0]
    d, hd = w1.shape[1], w1.shape[2]
    grid_spec = pltpu.PrefetchScalarGridSpec(
        num_scalar_prefetch=3,
        grid=(n_blocks,),
        in_specs=[
            pl.BlockSpec(memory_space=pl.ANY),
            pl.BlockSpec((1, d, hd), lambda i, be, nv, src: (be[i], 0, 0)),
            pl.BlockSpec((1, d, hd), lambda i, be, nv, src: (be[i], 0, 0)),
            pl.BlockSpec((1, hd, d), lambda i, be, nv, src: (be[i], 0, 0)),
        ],
        out_specs=pl.BlockSpec((MOE_BM, d), lambda i, be, nv, src: (i, 0)),
        scratch_shapes=[pltpu.VMEM((2 * MOE_BM * SUBLANES, LANES), F32),
                        pltpu.SemaphoreType.DMA((2,))],
    )
    return pl.pallas_call(
        _ffn_body,
        grid_spec=grid_spec,
        out_shape=jax.ShapeDtypeStruct((n_blocks * MOE_BM, d), F32),
        compiler_params=_cparams(("arbitrary",)),
    )(blk_expert, n_valid, src, h2_tiles, w1, w3, w2)


def _combine_body(d0_ref, d1_ref, w0_ref, w1_ref, y_hbm, x_ref, g_ref, o_ref, ya, yb, sem):
    i = pl.program_id(0)
    nsteps = pl.num_programs(0)
    rows = MOE_TC * SUBLANES

    def gather(step, slot, start):
        def body(r, carry):
            t = step * MOE_TC + r
            dst = pl.ds(pl.multiple_of(slot * rows + r * SUBLANES, SUBLANES), SUBLANES)
            for d_ref, buf, k in ((d0_ref, ya, 0), (d1_ref, yb, 1)):
                cp = pltpu.make_async_copy(
                    y_hbm.at[pl.ds(pl.multiple_of(d_ref[t] * SUBLANES, SUBLANES), SUBLANES), :],
                    buf.at[dst, :], sem.at[slot, k])
                if start:
                    cp.start()
                else:
                    cp.wait()
            return carry
        lax.fori_loop(0, MOE_TC, body, 0)

    @pl.when(i == 0)
    def _():
        gather(0, 0, True)

    @pl.when(i + 1 < nsteps)
    def _():
        gather(i + 1, (i + 1) % 2, True)

    slot = i % 2
    gather(i, slot, False)
    gate = g_ref[0]

    def body(r, carry):
        t = i * MOE_TC + r
        src = pl.ds(pl.multiple_of(slot * rows + r * SUBLANES, SUBLANES), SUBLANES)
        dst = pl.ds(pl.multiple_of(r * SUBLANES, SUBLANES), SUBLANES)
        o_ref[dst, :] = x_ref[dst, :] + gate * (w0_ref[t] * ya[src, :] + w1_ref[t] * yb[src, :])
        return carry
    lax.fori_loop(0, MOE_TC, body, 0)


def _moe_combine(dest, weights, y_tiles, x_tiles, gate_tiles, tok_per_seq, n_lat_seq):
    t = dest.shape[0]
    rows = MOE_TC * SUBLANES
    steps_per_seq = tok_per_seq // MOE_TC
    grid_spec = pltpu.PrefetchScalarGridSpec(
        num_scalar_prefetch=4,
        grid=(t // MOE_TC,),
        in_specs=[
            pl.BlockSpec(memory_space=pl.ANY),
            pl.BlockSpec((rows, LANES), lambda i, *_: (i, 0)),
            pl.BlockSpec((1, SUBLANES, LANES),
                         lambda i, *_: (jnp.minimum(i // steps_per_seq, n_lat_seq), 0, 0)),
        ],
        out_specs=pl.BlockSpec((rows, LANES), lambda i, *_: (i, 0)),
        scratch_shapes=[pltpu.VMEM((2 * rows, LANES), F32), pltpu.VMEM((2 * rows, LANES), F32),
                        pltpu.SemaphoreType.DMA((2, 2))],
    )
    return pl.pallas_call(
        _combine_body,
        grid_spec=grid_spec,
        out_shape=jax.ShapeDtypeStruct((t * SUBLANES, LANES), F32),
        compiler_params=_cparams(("arbitrary",)),
    )(dest[:, 0], dest[:, 1], weights[:, 0], weights[:, 1], y_tiles, x_tiles, gate_tiles)


def _final_norm_body(x_ref, g_ref, o_ref):
    x = x_ref[...]
    o_ref[...] = x * lax.rsqrt(jnp.mean(x * x, axis=-1, keepdims=True) + RMS_EPS) * g_ref[...]


def _final_norm(x, g):
    t, d = x.shape
    tm = 512
    return pl.pallas_call(
        _final_norm_body,
        grid=(t // tm,),
        in_specs=[pl.BlockSpec((tm, d), lambda i: (i, 0)), pl.BlockSpec((1, d), lambda i: (0, 0))],
        out_specs=pl.BlockSpec((tm, d), lambda i: (i, 0)),
        out_shape=jax.ShapeDtypeStruct((t, d), F32),
        compiler_params=_cparams(("arbitrary",)),
    )(x, g)


def _grid_pos_embed(n_tokens):
    rows = n_tokens // GRID_W
    quarter = D_MODEL // 4
    omega = 1.0 / (10000.0 ** (jnp.arange(quarter, dtype=F32) / quarter))
    ang_r = jnp.arange(rows, dtype=F32)[:, None] * omega
    ang_c = jnp.arange(GRID_W, dtype=F32)[:, None] * omega
    emb_r = jnp.concatenate([jnp.sin(ang_r), jnp.cos(ang_r)], axis=-1)
    emb_c = jnp.concatenate([jnp.sin(ang_c), jnp.cos(ang_c)], axis=-1)
    half = D_MODEL // 2
    pe = jnp.concatenate([jnp.broadcast_to(emb_r[:, None, :], (rows, GRID_W, half)),
                          jnp.broadcast_to(emb_c[None, :, :], (rows, GRID_W, half))], axis=-1)
    return pe.reshape(rows * GRID_W, D_MODEL)


def _mod_rows(m, n_parts):
    s = m.shape[0]
    parts = m.reshape(s, n_parts, D_MODEL)
    pad = jnp.zeros((s, 8, D_MODEL), F32)
    first = pad.at[:, :min(3, n_parts)].set(parts[:, :3])
    second = pad.at[:, :max(n_parts - 3, 0)].set(parts[:, 3:6]) if n_parts > 3 else pad
    return first, second


def kernel(x, c, ctx, c_ctx, w_mod, b_mod, norm1_g, norm2_g, w_in, ret_log_decay, ret_gn_g, s5_lambda_re, s5_lambda_im, s5_log_dt, s5_b_re, s5_b_im, s5_c_re, s5_c_im, s5_d, s5_glu_w1, s5_glu_w2, hy_short_w, hy_short_b, hy_f_w1, hy_f_b1, hy_f_w2, hy_f_b2, hy_f_w3, hy_f_freq, hy_bias, w_branch, w_out, moe_w_group, moe_b_group, moe_w_expert, moe_b_expert, moe_w1, moe_w3, moe_w2, final_norm_g):
    bsz, n_lat, d = x.shape
    n_ctx = ctx.shape[1]
    depth = w_in.shape[0]
    t_lat, t_ctx = bsz * n_lat, bsz * n_ctx

    xl = (x + _grid_pos_embed(n_lat)[None]).reshape(t_lat, d)
    xc = ctx.reshape(t_ctx, d)
    silu_c = jax.nn.silu(c)
    silu_cc = jax.nn.silu(c_ctx)[None]
    dft_lat = _dft_matrices(n_lat)
    dft_ctx = _dft_matrices(n_ctx)
    zero_state = jnp.zeros((bsz, RET_HEADS, RET_DK, RET_DK), F32)

    for l in range(depth):
        last = l == depth - 1
        mod_l = jnp.dot(silu_c, w_mod[l], precision=HI) + b_mod[l]
        n_ctx_mod = 2 if last else 6
        mod_c = jnp.dot(silu_cc, w_mod[l][:, :n_ctx_mod * d], precision=HI) + b_mod[l][:n_ctx_mod * d]
        m1l, m2l = _mod_rows(mod_l, 6)
        m1c, m2c = _mod_rows(mod_c, n_ctx_mod)

        w_in_b = w_in[l].astype(BF16)
        n1g = norm1_g[l][None]
        zl = _proj(xl, m1l, n1g, w_in_b, n_lat)
        zc = _proj(xc, m1c, n1g, w_in_b, t_ctx)

        tables = _ret_tables(ret_log_decay[l], min(RET_CHUNK, n_ctx))
        gn = ret_gn_g[l][None]
        yret_c, s_f, s_b = _retention(zc, bsz, n_ctx, tables, gn, zero_state, zero_state)
        yret_l, _, _ = _retention(zl, bsz, n_lat, tables, gn, s_f, s_b)

        ops = _s5_operators(s5_lambda_re[l], s5_lambda_im[l], s5_log_dt[l], s5_b_re[l], s5_b_im[l],
                            s5_c_re[l], s5_c_im[l])
        uc_g = _s5_to_groups(zc[:, COL_S5:COL_HY], bsz, n_ctx)
        ul_g = _s5_to_groups(zl[:, COL_S5:COL_HY], bsz, n_lat)
        yc_g, yl_g = _s5(uc_g, ul_g, ops)
        ys5_l = _s5_from_groups(yl_g, bsz, n_lat)

        sw = jnp.concatenate([hy_short_w[l], hy_short_b[l][None],
                              jnp.zeros((4, 3 * MIX_W), F32)], axis=0)
        filt_p = (hy_f_w1[l], hy_f_b1[l], hy_f_w2[l], hy_f_b2[l], hy_f_w3[l], hy_f_freq[l])
        hyb = hy_bias[l][None]
        x0c_l, yv_l = _hy_pre(zl, bsz, n_lat, sw)
        yhy_l = _hy_dft(yv_l, x0c_l, dft_lat, _filter_spectrum(_hyena_filter(n_lat, *filt_p)), hyb,
                        bsz, n_lat)

        wr = jnp.zeros((d, ROUTE_COLS), F32).at[:, :MOE_GROUPS].set(moe_w_group[l]) \
            .at[:, MOE_GROUPS:MOE_GROUPS + MOE_EXPERTS].set(moe_w_expert[l])
        br = jnp.zeros((1, ROUTE_COLS), F32).at[0, :MOE_GROUPS].set(moe_b_group[l]) \
            .at[0, MOE_GROUPS:MOE_GROUPS + MOE_EXPERTS].set(moe_b_expert[l])
        merge_w = (s5_d[l][None], s5_glu_w1[l].astype(BF16), s5_glu_w2[l].astype(BF16),
                   w_branch[l].astype(BF16), w_out[l].astype(BF16), norm2_g[l][None], wr, br)
        xl, h2_l, lg_l = _merge(yret_l, ys5_l, zl, yhy_l, xl, m1l, m2l, *merge_w, n_lat)

        if last:
            h2, lg, x_all, gates, t_all = h2_l, lg_l, xl, m2l[:, 2], t_lat
        else:
            ys5_c = _s5_from_groups(yc_g, bsz, n_ctx)
            x0c_c, yv_c = _hy_pre(zc, bsz, n_ctx, sw)
            yhy_c = _hy_dft(yv_c, x0c_c, dft_ctx, _filter_spectrum(_hyena_filter(n_ctx, *filt_p)), hyb,
                            bsz, n_ctx)
            xc, h2_c, lg_c = _merge(yret_c, ys5_c, zc, yhy_c, xc, m1c, m2c, *merge_w, t_ctx)
            h2 = jnp.concatenate([h2_l, h2_c], axis=0)
            lg = jnp.concatenate([lg_l, lg_c], axis=0)
            x_all = jnp.concatenate([xl, xc], axis=0)
            gates = jnp.concatenate([m2l[:, 2], m2c[:, 2]], axis=0)
            t_all = t_lat + t_ctx

        n_slots = t_all * MOE_TOPK
        rows_cap = (-(-n_slots // MOE_BM) + MOE_EXPERTS) * MOE_BM
        blk_expert, n_valid, src, dest, weights = _route(lg, rows_cap)
        y_sorted = _moe_ffn(blk_expert, n_valid, src, h2.reshape(t_all * SUBLANES, LANES),
                            moe_w1[l], moe_w3[l], moe_w2[l])
        x_new = _moe_combine(dest, weights, y_sorted.reshape(rows_cap * SUBLANES, LANES),
                             x_all.reshape(t_all * SUBLANES, LANES),
                             gates.reshape(-1, SUBLANES, LANES), n_lat, bsz).reshape(t_all, d)
        xl = x_new[:t_lat]
        if not last:
            xc = x_new[t_lat:]

    return _final_norm(xl, final_norm_g[None]).reshape(bsz, n_lat, d)
```
